```python
import jax, jax.numpy as jnp
from jax import lax
import numpy as np

D_MODEL = 1024
BATCH = 1
SEQ = 16384
DEPTH = 2
DEC_BATCH = 4
DEC_SEQ = 4096
PAST_LEN = 128

N_META = 16
CHUNK = 64
A_HEADS = 8
A_DK = 128
A_DV = D_MODEL // A_HEADS
A_K = A_HEADS * A_DK
A_V = A_HEADS * A_DV
B_HEADS = 4
B_DK = D_MODEL // 2 // B_HEADS
B_DV = D_MODEL // B_HEADS
B_K = B_HEADS * B_DK
B_V = B_HEADS * B_DV
B_RANK = 16
GATE_TEMP = 16.0
D_FF = 4 * D_MODEL
EPS = 1e-6
LB_FLOOR = 1e-30
SPLITS = (A_K, A_K, A_K, A_V, A_V, B_K, B_K, B_V, 2 * B_RANK, B_V, D_MODEL, D_MODEL)
PROJ_DIM = 3 * A_K + 2 * A_V + 2 * B_K + 2 * B_V + 2 * B_RANK + 2 * D_MODEL

kernel_name = "hgrn2_gla_gated_parallel_encoder"


def rms_norm(x, gain):
    xf = x.astype(jnp.float32)
    y = xf * lax.rsqrt(jnp.mean(xf * xf, axis=-1, keepdims=True) + EPS)
    return (y * gain.astype(jnp.float32)).astype(x.dtype)


def split_heads(t, n_heads):
    b, l, _ = t.shape
    return t.reshape(b, l, n_heads, -1).transpose(0, 2, 1, 3)


def merge_heads(t):
    b, h, l, d = t.shape
    return t.transpose(0, 2, 1, 3).reshape(b, l, h * d)


def chunk_gla(q, k, v, logg):
    out_dtype = v.dtype
    b, h, l, dk = q.shape
    dv = v.shape[-1]
    pad = (-l) % CHUNK
    padw = ((0, 0), (0, 0), (pad, 0), (0, 0))
    q, k, v, logg = [jnp.pad(t.astype(jnp.float32), padw) for t in (q, k, v, logg)]
    n = (l + pad) // CHUNK

    def to_chunks(t):
        return t.reshape(b, h, n, CHUNK, t.shape[-1]).transpose(2, 0, 1, 3, 4)

    qc, kc, vc = to_chunks(q), to_chunks(k), to_chunks(v)
    bc = jnp.cumsum(to_chunks(logg), axis=-2)
    causal = jnp.tril(jnp.ones((CHUNK, CHUNK), dtype=bool))[:, :, None]

    def step(state, inp):
        qi, ki, vi, bi = inp
        o = jnp.einsum('bhik,bhkv->bhiv', qi * jnp.exp(bi), state)
        diff = bi[:, :, :, None, :] - bi[:, :, None, :, :]
        dec = jnp.where(causal, jnp.exp(jnp.where(causal, diff, 0.0)), 0.0)
        scores = jnp.einsum('bhik,bhjk,bhijk->bhij', qi, ki, dec)
        o = o + jnp.einsum('bhij,bhjv->bhiv', scores, vi)
        b_last = bi[:, :, -1, :]
        state = jnp.exp(b_last)[..., None] * state + jnp.einsum(
            'bhjk,bhjv->bhkv', ki * jnp.exp(b_last[:, :, None, :] - bi), vi)
        return state, o

    s0 = jnp.zeros((b, h, dk, dv), jnp.float32)
    _, o = lax.scan(step, s0, (qc, kc, vc, bc))
    o = o.transpose(1, 2, 0, 3, 4).reshape(b, h, n * CHUNK, dv)[:, :, pad:]
    return o.astype(out_dtype)


def bidir_gla(q, k_fwd, k_bwd, v, lg_fwd, lg_bwd):
    rev = lambda t: jnp.flip(t, axis=2)
    fwd = chunk_gla(q, k_fwd, v, lg_fwd)
    bwd = rev(chunk_gla(rev(q), rev(k_bwd), rev(v), rev(lg_bwd)))
    return fwd + bwd


def gated_head_norm(o, gain, gate):
    of = o.astype(jnp.float32)
    of = of * lax.rsqrt(jnp.mean(of * of, axis=-1, keepdims=True) + EPS)
    y = merge_heads(of) * gain.astype(jnp.float32) * jax.nn.silu(gate.astype(jnp.float32))
    return y.astype(gate.dtype)


def hgrn2_log_forget(z, lb):
    return jnp.logaddexp(jnp.log(jnp.maximum(lb, LB_FLOOR)),
                         jnp.log1p(-lb) + jax.nn.log_sigmoid(z.astype(jnp.float32)))


def layer_lower_bounds(lb_logits):
    p = jax.nn.softmax(lb_logits.astype(jnp.float32), axis=0)
    return jnp.cumsum(p, axis=0) - p[0:1]


def mixer(xn, w_in, lb, w_gate, b_gate, norm_a, norm_b, w_out):
    p = jnp.einsum('bld,dp->blp', xn, w_in)
    cuts = [int(c) for c in np.cumsum(SPLITS)[:-1]]
    (q_a, zf_fwd, zf_bwd, i_a, g_a,
     q_g, k_g, v_g, r_g, g_g, m_a, m_b) = jnp.split(p, cuts, axis=-1)

    lf_fwd = hgrn2_log_forget(zf_fwd, lb[0])
    lf_bwd = hgrn2_log_forget(zf_bwd, lb[1])
    k_fwd = -jnp.expm1(lf_fwd)
    k_bwd = -jnp.expm1(lf_bwd)
    o_a = bidir_gla(split_heads(q_a, A_HEADS), split_heads(k_fwd, A_HEADS), split_heads(k_bwd, A_HEADS),
                    split_heads(i_a, A_HEADS), split_heads(lf_fwd, A_HEADS), split_heads(lf_bwd, A_HEADS))
    y_a = gated_head_norm(o_a, norm_a, g_a)

    r_fwd, r_bwd = jnp.split(r_g, 2, axis=-1)
    lg_fwd = jax.nn.log_sigmoid((r_fwd @ w_gate[0] + b_gate[0]).astype(jnp.float32)) / GATE_TEMP
    lg_bwd = jax.nn.log_sigmoid((r_bwd @ w_gate[1] + b_gate[1]).astype(jnp.float32)) / GATE_TEMP
    qh = split_heads(q_g, B_HEADS) * (B_DK ** -0.5)
    kh = split_heads(k_g, B_HEADS)
    o_b = bidir_gla(qh, kh, kh, split_heads(v_g, B_HEADS),
                    split_heads(lg_fwd, B_HEADS), split_heads(lg_bwd, B_HEADS))
    y_b = gated_head_norm(o_b, norm_b, g_g)

    u = jax.nn.sigmoid(m_a) * y_a + jax.nn.sigmoid(m_b) * y_b
    return jnp.einsum('bld,de->ble', u, w_out)


def trunk(x, meta_tokens, attn_norm, w_in, lb_logits, w_gate, b_gate, norm_a, norm_b,
          w_out, mlp_norm, w_up, w_down, final_norm):
    b = x.shape[0]
    meta = jnp.broadcast_to(meta_tokens[None].astype(x.dtype), (b, N_META, x.shape[-1]))
    h = jnp.concatenate([meta, x], axis=1)
    lbs = layer_lower_bounds(lb_logits)
    for l in range(DEPTH):
        h = h + mixer(rms_norm(h, attn_norm[l]), w_in[l], lbs[l], w_gate[l], b_gate[l],
                      norm_a[l], norm_b[l], w_out[l])
        hn = rms_norm(h, mlp_norm[l])
        h = h + jnp.einsum('blf,fd->bld', jnp.square(jax.nn.relu(jnp.einsum('bld,df->blf', hn, w_up[l]))), w_down[l])
    h = rms_norm(h, final_norm)
    return h[:, N_META:]


def setup_inputs(seed: int = 0) -> dict:
    key = jax.random.key(seed)
    ks = jax.random.split(key, 16)
    f32 = jnp.float32
    nrm = lambda k, s: jax.random.normal(k, s, f32)
    return {
        "x_prompt": nrm(ks[0], (BATCH, SEQ, D_MODEL)),
        "x_sample": nrm(ks[1], (DEC_BATCH, DEC_SEQ, D_MODEL)),
        "meta_tokens": nrm(ks[2], (N_META, D_MODEL)),
        "attn_norm": 1.0 + 0.02 * nrm(ks[3], (DEPTH, D_MODEL)),
        "w_in": nrm(ks[4], (DEPTH, D_MODEL, PROJ_DIM)) * D_MODEL ** -0.5,
        "lb_logits": 0.5 * nrm(ks[5], (DEPTH, 2, A_K)),
        "w_gate": nrm(ks[6], (DEPTH, 2, B_RANK, B_K)) * B_RANK ** -0.5,
        "b_gate": 0.1 * nrm(ks[7], (DEPTH, 2, B_K)),
        "norm_a": 1.0 + 0.02 * nrm(ks[8], (DEPTH, A_V)),
        "norm_b": 1.0 + 0.02 * nrm(ks[9], (DEPTH, B_V)),
        "w_out": nrm(ks[10], (DEPTH, D_MODEL, D_MODEL)) * D_MODEL ** -0.5,
        "mlp_norm": 1.0 + 0.02 * nrm(ks[11], (DEPTH, D_MODEL)),
        "w_up": nrm(ks[12], (DEPTH, D_MODEL, D_FF)) * D_MODEL ** -0.5,
        "w_down": nrm(ks[13], (DEPTH, D_FF, D_MODEL)) * D_FF ** -0.5,
        "final_norm": 1.0 + 0.02 * nrm(ks[14], (D_MODEL,)),
    }


def reference(x_prompt, x_sample, meta_tokens, attn_norm, w_in, lb_logits, w_gate, b_gate,
              norm_a, norm_b, w_out, mlp_norm, w_up, w_down, final_norm):
    y_prompt = trunk(x_prompt, meta_tokens, attn_norm, w_in, lb_logits, w_gate, b_gate,
                     norm_a, norm_b, w_out, mlp_norm, w_up, w_down, final_norm)
    y_sample = trunk(x_sample, meta_tokens, attn_norm, w_in, lb_logits, w_gate, b_gate,
                     norm_a, norm_b, w_out, mlp_norm, w_up, w_down, final_norm)
    return (y_prompt, y_sample)
```

```python
import functools

import numpy as np
import jax
import jax.numpy as jnp
from jax import lax
from jax.experimental import pallas as pl
from jax.experimental.pallas import tpu as pltpu

D_MODEL = 1024
N_META = 16
A_HEADS = 8
B_HEADS = 4
B_RANK = 16
HEAD_DK = 128
A_DV = 128
B_DV = 256
GATE_TEMP = 16.0
D_FF = 4 * D_MODEL
EPS = 1e-6
LB_FLOOR = 1e-30

LANES = 128
CHUNK = 128
N_LEVELS = 7
DIAG_LEVEL = N_LEVELS
NO_LEVEL = 99
TOKEN_TILE = 384
VMEM_LIMIT = 56 * 1024 * 1024

CB_QA, CB_ZF, CB_ZB, CB_IA, CB_GA, CB_GB, CB_MA, CB_MB, CB_VB, CB_QB, CB_KB, CB_R = (
    0, 8, 16, 24, 32, 40, 48, 56, 64, 72, 76, 80)
N_CB = 81
PROJ_GROUP = 9

_NT = (((1,), (1,)), ((), ()))
_TN = (((0,), (0,)), ((), ()))
_F32 = jnp.float32
_BF16 = jnp.bfloat16


def _dot(a, b, dims=None):
    if dims is None:
        return jnp.dot(a, b, preferred_element_type=_F32)
    return lax.dot_general(a, b, dims, preferred_element_type=_F32)


def _rms(x, gain):
    return x * lax.rsqrt(jnp.mean(x * x, axis=-1, keepdims=True) + EPS) * gain


def _proj_kernel(h_ref, gain_ref, w_ref, o_ref):
    xn = _rms(h_ref[...], gain_ref[...]).astype(_BF16)
    p = _dot(xn, w_ref[...])
    for cb in range(PROJ_GROUP):
        o_ref[cb] = p[:, cb * LANES:(cb + 1) * LANES].astype(_BF16)


def _proj(h, gain, w_packed):
    t = h.shape[0]
    n_groups = N_CB // PROJ_GROUP
    return pl.pallas_call(
        _proj_kernel,
        grid=(n_groups, t // TOKEN_TILE),
        in_specs=[
            pl.BlockSpec((TOKEN_TILE, D_MODEL), lambda g, i: (i, 0)),
            pl.BlockSpec((1, D_MODEL), lambda g, i: (0, 0)),
            pl.BlockSpec((D_MODEL, PROJ_GROUP * LANES), lambda g, i: (0, g)),
        ],
        out_specs=pl.BlockSpec((PROJ_GROUP, TOKEN_TILE, LANES), lambda g, i: (g, i, 0)),
        out_shape=jax.ShapeDtypeStruct((N_CB, t, LANES), _BF16),
        compiler_params=pltpu.CompilerParams(
            dimension_semantics=("arbitrary", "arbitrary"), vmem_limit_bytes=VMEM_LIMIT),
        name="proj",
    )(h, gain, w_packed)


def _level_codes(reverse):
    i = lax.broadcasted_iota(jnp.int32, (CHUNK, CHUNK), 0)
    j = lax.broadcasted_iota(jnp.int32, (CHUNK, CHUNK), 1)
    lvl = 31 - lax.clz(i ^ j)
    live = (i < j) if reverse else (i > j)
    return jnp.where(live, lvl, jnp.where(i == j, DIAG_LEVEL, NO_LEVEL))


def _scan_unit(q, k, v, g, st_ref, idx, lev, reverse):
    row = lax.broadcasted_iota(jnp.int32, (CHUNK, LANES), 0)
    q_bf = q.astype(_BF16)
    k_bf = k.astype(_BF16)
    scores = jnp.where(lev == DIAG_LEVEL, _dot(q_bf, k_bf, _NT), 0.0)
    a = g
    r = None
    y = g
    s = 1
    for level in range(N_LEVELS):
        qs = (q * a).astype(_BF16)
        ks = k_bf if r is None else (k * r).astype(_BF16)
        scores = jnp.where(lev == level, _dot(qs, ks, _NT), scores)
        y_dn = pltpu.roll(y, s, 0)
        y_up = pltpu.roll(y, CHUNK - s, 0)
        odd = (row & s) != 0
        if reverse:
            a = a * jnp.where(odd, 1.0, y_up)
            r_step = jnp.where(odd, y_dn, 1.0)
        else:
            a = a * jnp.where(odd, y_dn, 1.0)
            r_step = jnp.where(odd, 1.0, y_up)
        r = r_step if r is None else r * r_step
        y = y * jnp.where(odd, y_dn, y_up)
        s *= 2
    qc = (q * a).astype(_BF16)
    kc = (k * r).astype(_BF16)
    st = st_ref[idx]
    o = _dot(qc, st.astype(_BF16), _NT) + _dot(scores.astype(_BF16), v)
    st_ref[idx] = st * y[0:1, :] + _dot(v, kc, _TN)
    return o


def _sigmoid_pair(z):
    e = jnp.exp(-jnp.abs(z))
    inv = 1.0 / (1.0 + e)
    einv = e * inv
    pos = z >= 0
    return jnp.where(pos, inv, einv), jnp.where(pos, einv, inv)


def _hgrn2_gate(z, lb):
    sp, sn = _sigmoid_pair(z)
    lb_floor = jnp.maximum(lb, LB_FLOOR)
    f = lb_floor + (1.0 - lb) * sp
    k = (1.0 - lb) * sn - (lb_floor - lb)
    return f, k


def _gla_gate(x):
    ls = jnp.minimum(x, 0.0) - jnp.log1p(jnp.exp(-jnp.abs(x)))
    return jnp.exp(ls / GATE_TEMP)


def _lower_bound(lbl_ref, layer, depth, direction, head):
    rows = [lbl_ref[(l * 2 + direction) * A_HEADS + head] for l in range(depth)]
    m = functools.reduce(jnp.maximum, rows)
    es = [jnp.exp(x - m) for x in rows]
    tot = functools.reduce(lambda u, w: u + w, es)
    ps = [e / tot for e in es]
    cum = functools.reduce(lambda u, w: u + w, ps[:layer + 1])
    return cum - ps[0]


def _scan_kernel(starts, n_chunks, layer, depth,
                 qa_f, zf_f, ia_f, qb_f, kb_f, vb_f, r_f,
                 qa_b, zb_b, ia_b, qb_b, kb_b, vb_b, r_b,
                 lbl_ref, wg_ref, bg_ref,
                 oaf_ref, oab_ref, obf_ref, obb_ref,
                 sta_ref, stb_ref):
    c = pl.program_id(0)
    cb = n_chunks - 1 - c
    fwd_first = functools.reduce(jnp.logical_or, [c == s for s in starts])
    ends = [s - 1 for s in starts[1:]] + [n_chunks - 1]
    bwd_first = functools.reduce(jnp.logical_or, [cb == e for e in ends])

    @pl.when(fwd_first)
    def _():
        sta_ref[0] = jnp.zeros(sta_ref.shape[1:], _F32)
        stb_ref[0] = jnp.zeros(stb_ref.shape[1:], _F32)

    @pl.when(bwd_first)
    def _():
        sta_ref[1] = jnp.zeros(sta_ref.shape[1:], _F32)
        stb_ref[1] = jnp.zeros(stb_ref.shape[1:], _F32)

    lev_f = _level_codes(False)
    lev_b = _level_codes(True)

    def a_head(hd, carry):
        for direction, (q_ref, z_ref, v_ref, o_ref, lev) in enumerate(
                ((qa_f, zf_f, ia_f, oaf_ref, lev_f), (qa_b, zb_b, ia_b, oab_ref, lev_b))):
            lb = _lower_bound(lbl_ref, layer, depth, direction, hd)
            f, k = _hgrn2_gate(z_ref[hd].astype(_F32), lb)
            o_ref[hd] = _scan_unit(q_ref[hd].astype(_F32), k, v_ref[hd], f,
                                   sta_ref.at[direction], hd, lev, direction == 1)
        return carry

    lax.fori_loop(0, A_HEADS, a_head, 0)

    def b_head(hd, carry):
        for direction, (q_ref, k_ref, v_ref, r_ref, o_ref, lev) in enumerate(
                ((qb_f, kb_f, vb_f, r_f, obf_ref, lev_f), (qb_b, kb_b, vb_b, r_b, obb_ref, lev_b))):
            x = _dot(r_ref[0], wg_ref[direction * B_HEADS + hd]) + bg_ref[direction * B_HEADS + hd]
            g = _gla_gate(x)
            q = q_ref[hd].astype(_F32) * (HEAD_DK ** -0.5)
            v = jnp.concatenate([v_ref[2 * hd], v_ref[2 * hd + 1]], axis=-1)
            o = _scan_unit(q, k_ref[hd].astype(_F32), v, g,
                           stb_ref.at[direction], hd, lev, direction == 1)
            o_ref[2 * hd] = o[:, :LANES]
            o_ref[2 * hd + 1] = o[:, LANES:]
        return carry

    lax.fori_loop(0, B_HEADS, b_head, 0)


def _scan(p, lbl, wg, bg, starts, layer, depth):
    t = p.shape[1]
    n_chunks = t // CHUNK

    def spec(nblk, cb0, rev):
        blk = cb0 // nblk
        if rev:
            return pl.BlockSpec((nblk, CHUNK, LANES), lambda c: (blk, n_chunks - 1 - c, 0))
        return pl.BlockSpec((nblk, CHUNK, LANES), lambda c: (blk, c, 0))

    def whole(x):
        return pl.BlockSpec(x.shape, lambda c: (0,) * x.ndim)

    def side(rev, cb_z):
        return [spec(8, CB_QA, rev), spec(8, cb_z, rev), spec(8, CB_IA, rev),
                spec(4, CB_QB, rev), spec(4, CB_KB, rev), spec(8, CB_VB, rev), spec(1, CB_R, rev)]

    out_f = pl.BlockSpec((8, CHUNK, LANES), lambda c: (0, c, 0))
    out_b = pl.BlockSpec((8, CHUNK, LANES), lambda c: (0, n_chunks - 1 - c, 0))
    o_shape = jax.ShapeDtypeStruct((8, t, LANES), _F32)
    kernel = functools.partial(_scan_kernel, tuple(starts), n_chunks, layer, depth)
    return pl.pallas_call(
        kernel,
        grid=(n_chunks,),
        in_specs=side(False, CB_ZF) + side(True, CB_ZB) + [whole(lbl), whole(wg), whole(bg)],
        out_specs=[out_f, out_b, out_f, out_b],
        out_shape=[o_shape] * 4,
        scratch_shapes=[pltpu.VMEM((2, A_HEADS, A_DV, HEAD_DK), _F32),
                        pltpu.VMEM((2, B_HEADS, B_DV, HEAD_DK), _F32)],
        compiler_params=pltpu.CompilerParams(
            dimension_semantics=("arbitrary",), vmem_limit_bytes=VMEM_LIMIT),
        name="scan",
    )(*([p] * 14), lbl, wg, bg)


def _silu(x):
    return x / (1.0 + jnp.exp(-x))


def _sigmoid(x):
    return 1.0 / (1.0 + jnp.exp(-x))


def _post_kernel(last, h_ref, oaf_ref, oab_ref, obf_ref, obb_ref, ga_ref, gb_ref, ma_ref, mb_ref,
                 na_ref, nb_ref, wo_ref, mg_ref, wu_ref, wd_ref, fg_ref, out_ref):
    us = []
    for cb in range(A_HEADS):
        oa = oaf_ref[cb] + oab_ref[cb]
        ya = (oa * lax.rsqrt(jnp.mean(oa * oa, axis=-1, keepdims=True) + EPS)
              * na_ref[cb] * _silu(ga_ref[cb].astype(_F32)))
        hb = cb // 2
        ob0 = obf_ref[2 * hb] + obb_ref[2 * hb]
        ob1 = obf_ref[2 * hb + 1] + obb_ref[2 * hb + 1]
        ms = (jnp.sum(ob0 * ob0, axis=-1, keepdims=True)
              + jnp.sum(ob1 * ob1, axis=-1, keepdims=True)) * (1.0 / B_DV)
        ob = ob0 if cb % 2 == 0 else ob1
        yb = ob * lax.rsqrt(ms + EPS) * nb_ref[cb] * _silu(gb_ref[cb].astype(_F32))
        u = _sigmoid(ma_ref[cb].astype(_F32)) * ya + _sigmoid(mb_ref[cb].astype(_F32)) * yb
        us.append(u.astype(_BF16))
    u = jnp.concatenate(us, axis=-1)
    h1 = h_ref[...] + _dot(u, wo_ref[...])
    hn = _rms(h1, mg_ref[...]).astype(_BF16)
    acc = h1
    for j in range(D_FF // D_MODEL):
        up = _dot(hn, wu_ref[:, j * D_MODEL:(j + 1) * D_MODEL])
        act = jnp.square(jnp.maximum(up, 0.0)).astype(_BF16)
        acc = acc + _dot(act, wd_ref[j * D_MODEL:(j + 1) * D_MODEL, :])
    if last:
        acc = _rms(acc, fg_ref[...])
    out_ref[...] = acc


def _post(h, o4, p, na, nb, wo, mg, wu, wd, fg, last):
    t = h.shape[0]
    tile = pl.BlockSpec((TOKEN_TILE, D_MODEL), lambda i: (i, 0))
    o_spec = pl.BlockSpec((8, TOKEN_TILE, LANES), lambda i: (0, i, 0))

    def p_spec(cb0):
        return pl.BlockSpec((8, TOKEN_TILE, LANES), lambda i: (cb0 // 8, i, 0))

    def whole(x):
        return pl.BlockSpec(x.shape, lambda i: (0,) * x.ndim, pipeline_mode=pl.Buffered(1))

    return pl.pallas_call(
        functools.partial(_post_kernel, last),
        grid=(t // TOKEN_TILE,),
        in_specs=[tile, o_spec, o_spec, o_spec, o_spec,
                  p_spec(CB_GA), p_spec(CB_GB), p_spec(CB_MA), p_spec(CB_MB),
                  whole(na), whole(nb), whole(wo), whole(mg), whole(wu), whole(wd), whole(fg)],
        out_specs=tile,
        out_shape=jax.ShapeDtypeStruct((t, D_MODEL), _F32),
        compiler_params=pltpu.CompilerParams(
            dimension_semantics=("arbitrary",), vmem_limit_bytes=VMEM_LIMIT),
        name="post",
    )(h, *o4, p, p, p, p, na, nb, wo, mg, wu, wd, fg)


def _pack_w_in(w):
    cuts = np.cumsum([0, 1024, 1024, 1024, 1024, 1024, 512, 512, 1024, 2 * B_RANK, 1024, 1024, 1024])
    (qa, zf, zb, ia, ga, qb, kb, vb, rg, gb, ma, mb) = [w[:, cuts[n]:cuts[n + 1]] for n in range(12)]
    rg = jnp.pad(rg, ((0, 0), (0, LANES - 2 * B_RANK)))
    return jnp.concatenate([qa, zf, zb, ia, ga, gb, ma, mb, vb, qb, kb, rg], axis=1).astype(_BF16)


def _pack_gate(w_gate, b_gate):
    wg = jnp.zeros((2, LANES, B_HEADS * HEAD_DK), _F32)
    wg = wg.at[0, :B_RANK].set(w_gate[0]).at[1, B_RANK:2 * B_RANK].set(w_gate[1])
    wg = wg.reshape(2, LANES, B_HEADS, HEAD_DK).transpose(0, 2, 1, 3).reshape(2 * B_HEADS, LANES, HEAD_DK)
    bg = b_gate.reshape(2 * B_HEADS, 1, HEAD_DK)
    return wg.astype(_BF16), bg.astype(_F32)


def kernel(x_prompt, x_sample, meta_tokens, attn_norm, w_in, lb_logits, w_gate, b_gate,
           norm_a, norm_b, w_out, mlp_norm, w_up, w_down, final_norm):
    depth = w_in.shape[0]
    groups = (x_prompt, x_sample)
    pieces, starts, chunk0 = [], [], 0
    for x in groups:
        b, l, d = x.shape
        seq = N_META + l
        padded = -(-seq // CHUNK) * CHUNK
        meta = jnp.broadcast_to(meta_tokens[None].astype(x.dtype), (b, N_META, d))
        hseq = jnp.concatenate([meta, x, jnp.zeros((b, padded - seq, d), x.dtype)], axis=1)
        pieces.append(hseq.reshape(b * padded, d))
        for n in range(b):
            starts.append(chunk0 + n * (padded // CHUNK))
        chunk0 += b * (padded // CHUNK)
    t = chunk0 * CHUNK
    t_pad = -(-t // TOKEN_TILE) * TOKEN_TILE
    pieces.append(jnp.zeros((t_pad - t, D_MODEL), x_prompt.dtype))
    h = jnp.concatenate(pieces, axis=0).astype(_F32)

    lbl = lb_logits.astype(_F32).reshape(depth * 2 * A_HEADS, 1, HEAD_DK)
    fg = final_norm.astype(_F32).reshape(1, D_MODEL)
    for layer in range(depth):
        w_packed = _pack_w_in(w_in[layer])
        wg, bg = _pack_gate(w_gate[layer], b_gate[layer])
        p = _proj(h, attn_norm[layer].astype(_F32).reshape(1, D_MODEL), w_packed)
        o4 = _scan(p, lbl, wg, bg, starts, layer, depth)
        h = _post(h, o4, p,
                  norm_a[layer].astype(_F32).reshape(A_HEADS, 1, LANES),
                  norm_b[layer].astype(_F32).reshape(8, 1, LANES),
                  w_out[layer].astype(_BF16),
                  mlp_norm[layer].astype(_F32).reshape(1, D_MODEL),
                  w_up[layer].astype(_BF16), w_down[layer].astype(_BF16), fg,
                  layer == depth - 1)

    outs, row = [], 0
    for x in groups:
        b, l, d = x.shape
        padded = -(-(N_META + l) // CHUNK) * CHUNK
        y = h[row:row + b * padded].reshape(b, padded, d)[:, N_META:N_META + l]
        outs.append(y.astype(x.dtype))
        row += b * padded
    return tuple(outs)
```
